```python
import jax, jax.numpy as jnp
from jax import lax
import numpy as np

D_MODEL = 4096
BATCH = 4
SEQ = 2048
DEPTH = 2
DEC_BATCH = 128
DEC_SEQ = 4
PAST_LEN = 16384
PAGE_SIZE = 128

N_MIXERS = 2
N_REC = (DEPTH + 1) // 2
N_MLA = DEPTH // 2
LRU_WIDTH = D_MODEL
LRU_BLOCKS = 16
LRU_BLOCK_W = LRU_WIDTH // LRU_BLOCKS
CONV_W = 4
LRU_C = 8.0
N_HEADS = 32
QK_NOPE = 128
QK_ROPE = 64
V_DIM = 128
Q_LORA = 1024
KV_LORA = 512
GATE_W = N_HEADS * V_DIM
MLA_IN = Q_LORA + KV_LORA + QK_ROPE + GATE_W
ROPE_THETA = 10000.0
ATTN_SCALE = (QK_NOPE + QK_ROPE) ** -0.5
Q_BLOCK = 128
NORM_EPS = 1e-6

kernel_name = 'hybrid_rglru_mla_decode_step'


def rms_norm(x, g):
    xf = x.astype(jnp.float32)
    y = xf * lax.rsqrt(jnp.mean(xf * xf, axis=-1, keepdims=True) + NORM_EPS)
    return (y * g.astype(jnp.float32)).astype(x.dtype)


def rope_angles(pos):
    inv = ROPE_THETA ** (-jnp.arange(0, QK_ROPE, 2, dtype=jnp.float32) / QK_ROPE)
    ang = pos.astype(jnp.float32)[:, None] * inv[None, :]
    return jnp.cos(ang), jnp.sin(ang)


def apply_rope(x, cos, sin):
    xf = x.astype(jnp.float32)
    x1, x2 = jnp.split(xf, 2, axis=-1)
    return jnp.concatenate([x1 * cos - x2 * sin, x2 * cos + x1 * sin], axis=-1).astype(x.dtype)


def linear_recurrence(a, b, h0):
    def step(h, ab):
        a_t, b_t = ab
        h = a_t * h + b_t
        return h, h
    h_last, hs = lax.scan(step, h0, (a.swapaxes(0, 1), b.swapaxes(0, 1)))
    return hs.swapaxes(0, 1), h_last


def rglru_branch(h, conv_buf, h0, w_in, conv_w, conv_b, w_a, b_a, w_i, b_i, lam, w_out):
    B, S, _ = h.shape
    xb, gate = jnp.split(h @ w_in, 2, axis=-1)
    xpad = jnp.concatenate([conv_buf.astype(xb.dtype), xb], axis=1)
    xc = conv_b
    for k in range(CONV_W):
        xc = xc + xpad[:, k:k + S] * conv_w[k]
    new_buf = xpad[:, S:]
    xg = xc.reshape(B, S, LRU_BLOCKS, LRU_BLOCK_W)
    r = jax.nn.sigmoid((jnp.einsum('bsnk,nkj->bsnj', xg, w_a) + b_a).astype(jnp.float32)).reshape(B, S, LRU_WIDTH)
    i = jax.nn.sigmoid((jnp.einsum('bsnk,nkj->bsnj', xg, w_i) + b_i).astype(jnp.float32)).reshape(B, S, LRU_WIDTH)
    log_a = -LRU_C * r * jax.nn.softplus(-lam.astype(jnp.float32))
    a = jnp.exp(log_a)
    b = jnp.sqrt(-jnp.expm1(2.0 * log_a)) * i * xc.astype(jnp.float32)
    hs, h_last = linear_recurrence(a, b, h0.astype(jnp.float32))
    y = (hs.astype(h.dtype) * jax.nn.silu(gate)) @ w_out
    return y, h_last, new_buf


def mla_project(h, cos, sin, w_in, q_norm, kv_norm, w_qb):
    proj = h @ w_in
    cq, c_raw, kr_raw, gate = jnp.split(proj, [Q_LORA, Q_LORA + KV_LORA, Q_LORA + KV_LORA + QK_ROPE], axis=-1)
    q = jnp.einsum('bsc,chd->bshd', rms_norm(cq, q_norm), w_qb)
    q_nope, q_rope = jnp.split(q, [QK_NOPE], axis=-1)
    q_rope = apply_rope(q_rope, cos[:, None, :], sin[:, None, :])
    ckv = rms_norm(c_raw, kv_norm)
    k_rope = apply_rope(kr_raw, cos, sin)
    return q_nope, q_rope, ckv, k_rope, gate


def mla_attend_prompt(q_nope, q_rope, ckv, k_rope, w_uk, w_uv):
    B, S = ckv.shape[0], ckv.shape[1]
    k_nope = jnp.einsum('bsc,chn->bshn', ckv, w_uk)
    v = jnp.einsum('bsc,chv->bshv', ckv, w_uv)
    nqb = S // Q_BLOCK
    qn = q_nope.reshape(B, nqb, Q_BLOCK, N_HEADS, QK_NOPE).swapaxes(0, 1)
    qr = q_rope.reshape(B, nqb, Q_BLOCK, N_HEADS, QK_ROPE).swapaxes(0, 1)
    kpos = jnp.arange(S)

    def block(args):
        qn_b, qr_b, start = args
        s = (jnp.einsum('bqhn,bkhn->bhqk', qn_b, k_nope)
             + jnp.einsum('bqhr,bkr->bhqk', qr_b, k_rope)).astype(jnp.float32) * ATTN_SCALE
        qpos = start + jnp.arange(Q_BLOCK)
        s = jnp.where(kpos[None, :] <= qpos[:, None], s, -jnp.inf)
        p = jax.nn.softmax(s, axis=-1)
        return jnp.einsum('bhqk,bkhv->bqhv', p.astype(v.dtype), v)

    o = lax.map(block, (qn, qr, jnp.arange(nqb) * Q_BLOCK))
    return o.swapaxes(0, 1).reshape(B, S, N_HEADS, V_DIM)


def online_update(carry, s, vals):
    m, l, acc = carry
    m_new = jnp.maximum(m, jnp.max(s, axis=-1))
    alpha = jnp.exp(m - m_new)
    p = jnp.exp(s - m_new[..., None])
    l = alpha * l + jnp.sum(p, axis=-1)
    acc = alpha[..., None] * acc + jnp.einsum('bhqk,bkc->bhqc', p, vals.astype(jnp.float32))
    return (m_new, l, acc)


def mla_attend_sample(q_nope, q_rope, ckv_new, krope_new, cache_ckv, cache_krope, layer, page_table, w_uk, w_uv):
    DB, Sq = q_nope.shape[0], q_nope.shape[1]
    q_lat = jnp.einsum('bqhn,chn->bqhc', q_nope, w_uk)

    def scores(c, r):
        return (jnp.einsum('bqhc,bkc->bhqk', q_lat, c)
                + jnp.einsum('bqhr,bkr->bhqk', q_rope, r)).astype(jnp.float32) * ATTN_SCALE

    def page_step(carry, pages):
        c = cache_ckv[layer, pages]
        r = cache_krope[layer, pages]
        return online_update(carry, scores(c, r), c), None

    init = (jnp.full((DB, N_HEADS, Sq), -jnp.inf, jnp.float32),
            jnp.zeros((DB, N_HEADS, Sq), jnp.float32),
            jnp.zeros((DB, N_HEADS, Sq, KV_LORA), jnp.float32))
    carry, _ = lax.scan(page_step, init, page_table.T)
    causal = jnp.arange(Sq)[None, :] <= jnp.arange(Sq)[:, None]
    s_new = jnp.where(causal, scores(ckv_new, krope_new), -jnp.inf)
    m, l, acc = online_update(carry, s_new, ckv_new)
    out_lat = (acc / l[..., None]).astype(q_nope.dtype)
    return jnp.einsum('bhqc,chv->bqhv', out_lat, w_uv)


def mla_output(o, gate, w_out):
    B, S = o.shape[0], o.shape[1]
    return (o.reshape(B, S, GATE_W) * jax.nn.silu(gate)) @ w_out


def setup_inputs(seed: int = 0) -> dict:
    key = jax.random.key(seed)
    ks = iter(jax.random.split(key, 32))
    f32 = jnp.float32

    def nrm(shape, scale):
        return jax.random.normal(next(ks), shape, f32) * scale

    n_pages = PAST_LEN // PAGE_SIZE
    n_pool = (DEC_BATCH * n_pages * 5) // 4
    x_prompt = nrm((BATCH, SEQ, D_MODEL), 1.0)
    x_sample = nrm((DEC_BATCH, DEC_SEQ, D_MODEL), 1.0)
    cache_ckv = nrm((N_MLA, n_pool, PAGE_SIZE, KV_LORA), 1.0)
    cache_krope = nrm((N_MLA, n_pool, PAGE_SIZE, QK_ROPE), 1.0)
    page_table = jax.random.permutation(next(ks), n_pool)[:DEC_BATCH * n_pages].reshape(DEC_BATCH, n_pages).astype(jnp.int32)
    state_h = nrm((N_REC, DEC_BATCH, LRU_WIDTH), 0.5)
    state_conv = nrm((N_REC, DEC_BATCH, CONV_W - 1, LRU_WIDTH), 1.0)
    norm_pre = 1.0 + nrm((DEPTH, D_MODEL), 0.02)
    norm_post = 1.0 + nrm((DEPTH, D_MODEL), 0.02)
    rec_w_in = nrm((N_REC, D_MODEL, 2 * LRU_WIDTH), D_MODEL ** -0.5)
    rec_conv_w = nrm((N_REC, CONV_W, LRU_WIDTH), CONV_W ** -0.5)
    rec_conv_b = nrm((N_REC, LRU_WIDTH), 0.02)
    rec_w_a = nrm((N_REC, LRU_BLOCKS, LRU_BLOCK_W, LRU_BLOCK_W), LRU_BLOCK_W ** -0.5)
    rec_b_a = nrm((N_REC, LRU_BLOCKS, LRU_BLOCK_W), 0.02)
    rec_w_i = nrm((N_REC, LRU_BLOCKS, LRU_BLOCK_W, LRU_BLOCK_W), LRU_BLOCK_W ** -0.5)
    rec_b_i = nrm((N_REC, LRU_BLOCKS, LRU_BLOCK_W), 0.02)
    u = jax.random.uniform(next(ks), (N_REC, LRU_WIDTH), f32, minval=0.9, maxval=0.999)
    sig = u ** (1.0 / LRU_C)
    rec_lambda = jnp.log(sig) - jnp.log1p(-sig)
    rec_w_out = nrm((N_REC, LRU_WIDTH, D_MODEL), LRU_WIDTH ** -0.5)
    mla_w_in = nrm((N_MLA, D_MODEL, MLA_IN), D_MODEL ** -0.5)
    mla_q_norm = 1.0 + nrm((N_MLA, Q_LORA), 0.02)
    mla_kv_norm = 1.0 + nrm((N_MLA, KV_LORA), 0.02)
    mla_w_qb = nrm((N_MLA, Q_LORA, N_HEADS, QK_NOPE + QK_ROPE), Q_LORA ** -0.5)
    mla_w_uk = nrm((N_MLA, KV_LORA, N_HEADS, QK_NOPE), KV_LORA ** -0.5)
    mla_w_uv = nrm((N_MLA, KV_LORA, N_HEADS, V_DIM), KV_LORA ** -0.5)
    mla_w_out = nrm((N_MLA, GATE_W, D_MODEL), GATE_W ** -0.5)
    return {'x_prompt': x_prompt, 'x_sample': x_sample, 'cache_ckv': cache_ckv, 'cache_krope': cache_krope,
            'page_table': page_table, 'state_h': state_h, 'state_conv': state_conv,
            'norm_pre': norm_pre, 'norm_post': norm_post, 'rec_w_in': rec_w_in, 'rec_conv_w': rec_conv_w,
            'rec_conv_b': rec_conv_b, 'rec_w_a': rec_w_a, 'rec_b_a': rec_b_a, 'rec_w_i': rec_w_i,
            'rec_b_i': rec_b_i, 'rec_lambda': rec_lambda, 'rec_w_out': rec_w_out, 'mla_w_in': mla_w_in,
            'mla_q_norm': mla_q_norm, 'mla_kv_norm': mla_kv_norm, 'mla_w_qb': mla_w_qb, 'mla_w_uk': mla_w_uk,
            'mla_w_uv': mla_w_uv, 'mla_w_out': mla_w_out}


def reference(x_prompt, x_sample, cache_ckv, cache_krope, page_table, state_h, state_conv,
              norm_pre, norm_post, rec_w_in, rec_conv_w, rec_conv_b, rec_w_a, rec_b_a, rec_w_i, rec_b_i,
              rec_lambda, rec_w_out, mla_w_in, mla_q_norm, mla_kv_norm, mla_w_qb, mla_w_uk, mla_w_uv, mla_w_out):
    cos_p, sin_p = rope_angles(jnp.arange(x_prompt.shape[1]))
    cos_s, sin_s = rope_angles(PAST_LEN + jnp.arange(x_sample.shape[1]))
    xp, xs = x_prompt, x_sample
    ckv_p, kr_p, ckv_s, kr_s = [], [], [], []
    h_p, cv_p, h_s, cv_s = [], [], [], []
    for i in range(DEPTH):
        j = i // N_MIXERS
        hp = rms_norm(xp, norm_pre[i])
        hs = rms_norm(xs, norm_pre[i])
        if i % N_MIXERS == 0:
            zero_buf = jnp.zeros((xp.shape[0], CONV_W - 1, LRU_WIDTH), xp.dtype)
            zero_h = jnp.zeros((xp.shape[0], LRU_WIDTH), jnp.float32)
            yp, hl_p, buf_p = rglru_branch(hp, zero_buf, zero_h, rec_w_in[j], rec_conv_w[j], rec_conv_b[j],
                                           rec_w_a[j], rec_b_a[j], rec_w_i[j], rec_b_i[j], rec_lambda[j], rec_w_out[j])
            ys, hl_s, buf_s = rglru_branch(hs, state_conv[j], state_h[j], rec_w_in[j], rec_conv_w[j], rec_conv_b[j],
                                           rec_w_a[j], rec_b_a[j], rec_w_i[j], rec_b_i[j], rec_lambda[j], rec_w_out[j])
            h_p.append(hl_p)
            cv_p.append(buf_p)
            h_s.append(hl_s)
            cv_s.append(buf_s)
        else:
            qn_p, qr_p, c_p, r_p, g_p = mla_project(hp, cos_p, sin_p, mla_w_in[j], mla_q_norm[j], mla_kv_norm[j], mla_w_qb[j])
            o_p = mla_attend_prompt(qn_p, qr_p, c_p, r_p, mla_w_uk[j], mla_w_uv[j])
            qn_s, qr_s, c_s, r_s, g_s = mla_project(hs, cos_s, sin_s, mla_w_in[j], mla_q_norm[j], mla_kv_norm[j], mla_w_qb[j])
            o_s = mla_attend_sample(qn_s, qr_s, c_s, r_s, cache_ckv, cache_krope, j, page_table, mla_w_uk[j], mla_w_uv[j])
            yp = mla_output(o_p, g_p, mla_w_out[j])
            ys = mla_output(o_s, g_s, mla_w_out[j])
            ckv_p.append(c_p)
            kr_p.append(r_p)
            ckv_s.append(c_s)
            kr_s.append(r_s)
        xp = xp + rms_norm(yp, norm_post[i])
        xs = xs + rms_norm(ys, norm_post[i])
    return (xp, xs, jnp.stack(ckv_p), jnp.stack(kr_p), jnp.stack(ckv_s), jnp.stack(kr_s),
            jnp.stack(h_p), jnp.stack(cv_p), jnp.stack(h_s), jnp.stack(cv_s))
```

```python
import functools

import jax
import jax.numpy as jnp
from jax import lax
from jax.experimental import pallas as pl
from jax.experimental.pallas import tpu as pltpu

F32 = jnp.float32
BF16 = jnp.bfloat16

D_MODEL = 4096
LRU_WIDTH = 4096
LRU_BLOCKS = 16
LRU_BLOCK_W = LRU_WIDTH // LRU_BLOCKS
CONV_W = 4
LRU_C = 8.0
N_HEADS = 32
QK_NOPE = 128
QK_ROPE = 64
V_DIM = 128
Q_LORA = 1024
KV_LORA = 512
GATE_W = N_HEADS * V_DIM
PAST_LEN = 16384
PAGE_SIZE = 128
ROPE_THETA = 10000.0
ATTN_SCALE = (QK_NOPE + QK_ROPE) ** -0.5
NORM_EPS = 1e-6
HEAD_QW = QK_NOPE + 2 * QK_ROPE

VMEM_LIMIT_BYTES = 56 * 1024 * 1024
SUBLANES = 8
NEG_INF = float("-inf")


def _params(*sem):
    return pltpu.CompilerParams(dimension_semantics=sem, vmem_limit_bytes=VMEM_LIMIT_BYTES)


def _silu(x):
    return x * jax.nn.sigmoid(x)


def _rms(x, g):
    ms = jnp.mean(x * x, axis=-1, keepdims=True)
    return x * lax.rsqrt(ms + NORM_EPS) * g


def _rmsnorm_kernel(x_ref, g_ref, o_ref):
    o_ref[...] = _rms(x_ref[...], g_ref[...]).astype(o_ref.dtype)


def rmsnorm(x, g, out_dtype, tm=256):
    m, d = x.shape
    return pl.pallas_call(
        _rmsnorm_kernel,
        grid=(m // tm,),
        in_specs=[pl.BlockSpec((tm, d), lambda i: (i, 0)),
                  pl.BlockSpec((1, d), lambda i: (0, 0))],
        out_specs=pl.BlockSpec((tm, d), lambda i: (i, 0)),
        out_shape=jax.ShapeDtypeStruct((m, d), out_dtype),
        compiler_params=_params("parallel"),
        name="rmsnorm",
    )(x, g.reshape(1, d))


def _norm_residual_kernel(y_ref, g_ref, x_ref, o_ref):
    o_ref[...] = x_ref[...] + _rms(y_ref[...], g_ref[...])


def norm_residual(y, g, x, tm=256):
    m, d = x.shape
    row = pl.BlockSpec((tm, d), lambda i: (i, 0))
    return pl.pallas_call(
        _norm_residual_kernel,
        grid=(m // tm,),
        in_specs=[row, pl.BlockSpec((1, d), lambda i: (0, 0)), row],
        out_specs=row,
        out_shape=jax.ShapeDtypeStruct((m, d), F32),
        compiler_params=_params("parallel"),
        name="norm_residual",
    )(y, g.reshape(1, d), x)


def _mm_kernel(x_ref, w_ref, o_ref):
    x = x_ref[...].astype(BF16)
    o_ref[...] = jnp.dot(x, w_ref[...], preferred_element_type=F32).astype(o_ref.dtype)


def _mm_table_kernel(x_ref, w_ref, tab_ref, o_ref):
    x = x_ref[...].astype(BF16)
    acc = jnp.dot(x, w_ref[...], preferred_element_type=F32)
    tab = tab_ref[...]
    for j in range(acc.shape[1] // HEAD_QW):
        cols = slice(j * HEAD_QW, (j + 1) * HEAD_QW)
        o_ref[:, cols] = (acc[:, cols] * tab).astype(o_ref.dtype)


def matmul(x, w, out_dtype, tm, tn, table=None, name="matmul"):
    m, k = x.shape
    n = w.shape[1]
    in_specs = [pl.BlockSpec((tm, k), lambda i, j: (i, 0)),
                pl.BlockSpec((k, tn), lambda i, j: (0, j))]
    args = [x, w]
    kernel = _mm_kernel
    if table is not None:
        n_tab = table.shape[0] // tm
        in_specs.append(pl.BlockSpec((tm, HEAD_QW), lambda i, j: (i % n_tab, 0)))
        args.append(table)
        kernel = _mm_table_kernel
    return pl.pallas_call(
        kernel,
        grid=(m // tm, n // tn),
        in_specs=in_specs,
        out_specs=pl.BlockSpec((tm, tn), lambda i, j: (i, j)),
        out_shape=jax.ShapeDtypeStruct((m, n), out_dtype),
        compiler_params=_params("parallel", "parallel"),
        name=name,
    )(*args)


def _lru_coeffs(xc, wa_ref, ba_ref, wi_ref, bi_ref, lam_ref):
    xcb = xc.astype(BF16)
    r = jax.nn.sigmoid(jnp.dot(xcb, wa_ref[...], preferred_element_type=F32) + ba_ref[...])
    i = jax.nn.sigmoid(jnp.dot(xcb, wi_ref[...], preferred_element_type=F32) + bi_ref[...])
    log_a = -LRU_C * r * jax.nn.softplus(-lam_ref[...])
    a = jnp.exp(log_a)
    b = jnp.sqrt(-jnp.tanh(log_a) * (a * a + 1.0)) * i * xc
    return a, b


def _rglru_prompt_kernel(xb_ref, gate_ref, cw_ref, cb_ref, wa_ref, ba_ref, wi_ref, bi_ref, lam_ref,
                         z_ref, hlast_ref, xpad_s, hcar_s):
    t = pl.program_id(2)
    tc = xb_ref.shape[0]
    tail = CONV_W - 1

    @pl.when(t == 0)
    def _():
        xpad_s[0:SUBLANES, :] = jnp.zeros((SUBLANES, LRU_BLOCK_W), F32)
        hcar_s[...] = jnp.zeros_like(hcar_s)

    xb = xb_ref[...]
    xpad_s[SUBLANES:, :] = xb
    xc = cb_ref[...]
    for k in range(CONV_W):
        xc = xc + xpad_s[SUBLANES - tail + k:SUBLANES - tail + k + tc, :] * cw_ref[k:k + 1, :]
    xpad_s[0:SUBLANES, :] = xb[tc - SUBLANES:, :]

    a, b = _lru_coeffs(xc, wa_ref, ba_ref, wi_ref, bi_ref, lam_ref)

    row = lax.broadcasted_iota(jnp.int32, a.shape, 0) & (SUBLANES - 1)
    d = 1
    while d < SUBLANES:
        keep = row >= d
        a_prev = jnp.where(keep, pltpu.roll(a, d, 0), 1.0)
        b_prev = jnp.where(keep, pltpu.roll(b, d, 0), 0.0)
        b = a * b_prev + b
        a = a * a_prev
        d *= 2

    h = hcar_s[...]
    groups = []
    for g in range(tc // SUBLANES):
        rows = slice(g * SUBLANES, (g + 1) * SUBLANES)
        hg = a[rows] * h + b[rows]
        groups.append(hg)
        h = hg[SUBLANES - 1:SUBLANES]
    hs = jnp.concatenate(groups, axis=0)
    hcar_s[...] = h
    z_ref[...] = (hs * _silu(gate_ref[...])).astype(z_ref.dtype)

    @pl.when(t == pl.num_programs(2) - 1)
    def _():
        hlast_ref[...] = h


def rglru_prompt(proj, cw, cb, wa, ba, wi, bi, lam, tc=256):
    bsz, s, _ = proj.shape
    nb, bw = LRU_BLOCKS, LRU_BLOCK_W
    vec = lambda rows: pl.BlockSpec((rows, bw), lambda b, n, t: (0, n))
    blk = lambda rows: pl.BlockSpec((None, rows, bw), lambda b, n, t: (n, 0, 0))
    z, hlast = pl.pallas_call(
        _rglru_prompt_kernel,
        grid=(bsz, nb, s // tc),
        in_specs=[pl.BlockSpec((None, tc, bw), lambda b, n, t: (b, t, n)),
                  pl.BlockSpec((None, tc, bw), lambda b, n, t: (b, t, nb + n)),
                  vec(CONV_W), vec(1), blk(bw), blk(1), blk(bw), blk(1), vec(1)],
        out_specs=[pl.BlockSpec((None, tc, bw), lambda b, n, t: (b, t, n)),
                   pl.BlockSpec((None, 1, bw), lambda b, n, t: (b, 0, n))],
        out_shape=[jax.ShapeDtypeStruct((bsz, s, LRU_WIDTH), BF16),
                   jax.ShapeDtypeStruct((bsz, 1, LRU_WIDTH), F32)],
        scratch_shapes=[pltpu.VMEM((tc + SUBLANES, bw), F32), pltpu.VMEM((1, bw), F32)],
        compiler_params=_params("parallel", "parallel", "arbitrary"),
        name="rglru_prompt",
    )(proj, proj, cw, cb, wa, ba, wi, bi, lam)
    return z, hlast.reshape(bsz, LRU_WIDTH)


def _rglru_sample_kernel(xb_ref, gate_ref, cs_ref, h0_ref, cw_ref, cb_ref, wa_ref, ba_ref, wi_ref,
                         bi_ref, lam_ref, z_ref, hlast_ref):
    sq, db, bw = xb_ref.shape
    xpad = [cs_ref[k] for k in range(CONV_W - 1)] + [xb_ref[t] for t in range(sq)]
    xcs = []
    for t in range(sq):
        xc = cb_ref[...]
        for k in range(CONV_W):
            xc = xc + xpad[t + k] * cw_ref[k:k + 1, :]
        xcs.append(xc)
    a, b = _lru_coeffs(jnp.concatenate(xcs, axis=0), wa_ref, ba_ref, wi_ref, bi_ref, lam_ref)
    h = h0_ref[...]
    for t in range(sq):
        rows = slice(t * db, (t + 1) * db)
        h = a[rows] * h + b[rows]
        z_ref[t] = (h * _silu(gate_ref[t])).astype(z_ref.dtype)
    hlast_ref[...] = h


def rglru_sample(proj, conv_state, h0, cw, cb, wa, ba, wi, bi, lam):
    sq, db, _ = proj.shape
    nb, bw = LRU_BLOCKS, LRU_BLOCK_W
    vec = lambda rows: pl.BlockSpec((rows, bw), lambda n: (0, n))
    blk = lambda rows: pl.BlockSpec((None, rows, bw), lambda n: (n, 0, 0))
    return pl.pallas_call(
        _rglru_sample_kernel,
        grid=(nb,),
        in_specs=[pl.BlockSpec((sq, db, bw), lambda n: (0, 0, n)),
                  pl.BlockSpec((sq, db, bw), lambda n: (0, 0, nb + n)),
                  pl.BlockSpec((CONV_W - 1, db, bw), lambda n: (0, 0, n)),
                  vec(db), vec(CONV_W), vec(1), blk(bw), blk(1), blk(bw), blk(1), vec(1)],
        out_specs=[pl.BlockSpec((sq, db, bw), lambda n: (0, 0, n)), vec(db)],
        out_shape=[jax.ShapeDtypeStruct((sq, db, LRU_WIDTH), BF16),
                   jax.ShapeDtypeStruct((db, LRU_WIDTH), F32)],
        compiler_params=_params("parallel"),
        name="rglru_sample",
    )(proj, proj, conv_state, h0, cw, cb, wa, ba, wi, bi, lam)


def _mla_mid_kernel(p_ref, qn_ref, kvn_ref, tab_ref, cq_ref, ckv_ref, kr_ref, kr2_ref):
    cq_ref[...] = _rms(p_ref[:, :Q_LORA], qn_ref[...]).astype(cq_ref.dtype)
    ckv_ref[...] = _rms(p_ref[:, Q_LORA:Q_LORA + KV_LORA], kvn_ref[...])
    parts = p_ref[:, Q_LORA + KV_LORA:] * tab_ref[...]
    both = parts + pltpu.roll(parts, QK_ROPE, 1)
    kr_ref[...] = both[:, :QK_ROPE]
    kr2_ref[...] = both.astype(kr2_ref.dtype)


def mla_mid(p, q_norm, kv_norm, tab, tm=256):
    m, w = p.shape
    n_tab = tab.shape[0] // tm
    row = lambda width: pl.BlockSpec((tm, width), lambda i: (i, 0))
    return pl.pallas_call(
        _mla_mid_kernel,
        grid=(m // tm,),
        in_specs=[row(w),
                  pl.BlockSpec((1, Q_LORA), lambda i: (0, 0)),
                  pl.BlockSpec((1, KV_LORA), lambda i: (0, 0)),
                  pl.BlockSpec((tm, 2 * QK_ROPE), lambda i: (i % n_tab, 0))],
        out_specs=[row(Q_LORA), row(KV_LORA), row(QK_ROPE), row(2 * QK_ROPE)],
        out_shape=[jax.ShapeDtypeStruct((m, Q_LORA), BF16),
                   jax.ShapeDtypeStruct((m, KV_LORA), F32),
                   jax.ShapeDtypeStruct((m, QK_ROPE), F32),
                   jax.ShapeDtypeStruct((m, 2 * QK_ROPE), BF16)],
        compiler_params=_params("parallel"),
        name="mla_mid",
    )(p, q_norm.reshape(1, Q_LORA), kv_norm.reshape(1, KV_LORA), tab)


def _prompt_attn_kernel(q_ref, kn_ref, kr2_ref, v_ref, gate_ref, z_ref, k_s, m_s, l_s, acc_s, *, tq):
    s = q_ref.shape[0]
    nq = s // tq
    k_s[:, :QK_NOPE] = kn_ref[...]
    k_s[:, QK_NOPE:] = kr2_ref[...]

    def block(qi, kj, masked):
        q = q_ref[pl.ds(pl.multiple_of(qi * tq, tq), tq), :]
        kstart = pl.multiple_of(kj * tq, tq)
        k = k_s[pl.ds(kstart, tq), :]
        v = v_ref[pl.ds(kstart, tq), :]
        sc = lax.dot_general(q, k, (((1,), (1,)), ((), ())), preferred_element_type=F32) * ATTN_SCALE
        if masked:
            qpos = lax.broadcasted_iota(jnp.int32, sc.shape, 0)
            kpos = lax.broadcasted_iota(jnp.int32, sc.shape, 1)
            sc = jnp.where(kpos <= qpos, sc, NEG_INF)
        m_old = m_s[...]
        m_new = jnp.maximum(m_old, jnp.max(sc, axis=-1, keepdims=True))
        alpha = jnp.exp(m_old - m_new)
        p = jnp.exp(sc - m_new)
        l_s[...] = alpha * l_s[...] + jnp.sum(p, axis=-1, keepdims=True)
        acc_s[...] = alpha * acc_s[...] + jnp.dot(p.astype(BF16), v, preferred_element_type=F32)
        m_s[...] = m_new

    def q_block(qi, carry):
        m_s[...] = jnp.full_like(m_s, NEG_INF)
        l_s[...] = jnp.zeros_like(l_s)
        acc_s[...] = jnp.zeros_like(acc_s)

        def kv_block(kj, c):
            block(qi, kj, masked=False)
            return c

        lax.fori_loop(0, qi, kv_block, 0)
        block(qi, qi, masked=True)
        rows = pl.ds(pl.multiple_of(qi * tq, tq), tq)
        o = acc_s[...] / l_s[...]
        z_ref[rows, :] = (o * _silu(gate_ref[rows, :])).astype(z_ref.dtype)
        return carry

    lax.fori_loop(0, nq, q_block, 0)


def prompt_attention(q, kv, kr2, gate, tq=256):
    bsz, s, _ = q.shape
    head = lambda width, off: pl.BlockSpec((None, s, width), lambda b, h: (b, 0, off + h))
    return pl.pallas_call(
        functools.partial(_prompt_attn_kernel, tq=tq),
        grid=(bsz, N_HEADS),
        in_specs=[head(HEAD_QW, 0), head(QK_NOPE, 0),
                  pl.BlockSpec((None, s, 2 * QK_ROPE), lambda b, h: (b, 0, 0)),
                  head(V_DIM, N_HEADS), head(V_DIM, 0)],
        out_specs=head(V_DIM, 0),
        out_shape=jax.ShapeDtypeStruct((bsz, s, GATE_W), BF16),
        scratch_shapes=[pltpu.VMEM((s, HEAD_QW), BF16), pltpu.VMEM((tq, 1), F32),
                        pltpu.VMEM((tq, 1), F32), pltpu.VMEM((tq, V_DIM), F32)],
        compiler_params=_params("parallel", "parallel"),
        name="prompt_attention",
    )(q, kv, kr2, kv, gate)


def _head_in_kernel(x_ref, w_ref, o_ref):
    o_ref[...] = jnp.dot(x_ref[...], w_ref[...], preferred_element_type=F32).astype(o_ref.dtype)


def absorb_queries(q, w_ukt):
    m = q.shape[0]
    return pl.pallas_call(
        _head_in_kernel,
        grid=(N_HEADS,),
        in_specs=[pl.BlockSpec((m, QK_NOPE), lambda h: (0, (HEAD_QW // QK_NOPE) * h)),
                  pl.BlockSpec((None, QK_NOPE, KV_LORA), lambda h: (h, 0, 0))],
        out_specs=pl.BlockSpec((m, KV_LORA), lambda h: (0, h)),
        out_shape=jax.ShapeDtypeStruct((m, N_HEADS * KV_LORA), BF16),
        compiler_params=_params("parallel"),
        name="absorb_queries",
    )(q, w_ukt)


def _head_out_kernel(x_ref, w_ref, gate_ref, z_ref):
    o = jnp.dot(x_ref[...], w_ref[...], preferred_element_type=F32)
    z_ref[...] = (o * _silu(gate_ref[...])).astype(z_ref.dtype)


def expand_values(out_lat, w_uvh, gate):
    m = out_lat.shape[0]
    return pl.pallas_call(
        _head_out_kernel,
        grid=(N_HEADS,),
        in_specs=[pl.BlockSpec((m, KV_LORA), lambda h: (0, h)),
                  pl.BlockSpec((None, KV_LORA, V_DIM), lambda h: (h, 0, 0)),
                  pl.BlockSpec((m, V_DIM), lambda h: (0, h))],
        out_specs=pl.BlockSpec((m, V_DIM), lambda h: (0, h)),
        out_shape=jax.ShapeDtypeStruct((m, GATE_W), BF16),
        compiler_params=_params("parallel"),
        name="expand_values",
    )(out_lat, w_uvh, gate)


def _sample_attn_kernel(pt_ref, qlat_ref, qr_ref, cnew_ref, rnew_ref, *rest, pages_per_step):
    del pt_ref
    ckv_refs = rest[:pages_per_step]
    kr_refs = rest[pages_per_step:2 * pages_per_step]
    o_ref, qr_s, m_s, l_s, acc_s = rest[2 * pages_per_step:]
    g = pl.program_id(1)
    sq, nh, _ = qlat_ref.shape
    rows = sq * nh

    @pl.when(g == 0)
    def _():
        parts = qr_ref[...].reshape(rows, 2 * QK_ROPE).astype(F32)
        qr_s[...] = (parts[:, :QK_ROPE] + parts[:, QK_ROPE:]).astype(qr_s.dtype)
        m_s[...] = jnp.full_like(m_s, NEG_INF)
        l_s[...] = jnp.zeros_like(l_s)
        acc_s[...] = jnp.zeros_like(acc_s)

    qlat = qlat_ref[...].reshape(rows, KV_LORA)
    qr = qr_s[...]

    def update(c, r, mask):
        contract_last = (((1,), (1,)), ((), ()))
        sc = (lax.dot_general(qlat, c, contract_last, preferred_element_type=F32)
              + lax.dot_general(qr, r, contract_last, preferred_element_type=F32)) * ATTN_SCALE
        if mask is not None:
            sc = jnp.where(mask, sc, NEG_INF)
        m_old = m_s[...]
        m_new = jnp.maximum(m_old, jnp.max(sc, axis=-1, keepdims=True))
        alpha = jnp.exp(m_old - m_new)
        p = jnp.exp(sc - m_new)
        l_s[...] = alpha * l_s[...] + jnp.sum(p, axis=-1, keepdims=True)
        acc_s[...] = alpha * acc_s[...] + jnp.dot(p.astype(BF16), c, preferred_element_type=F32)
        m_s[...] = m_new

    c_all = jnp.concatenate([ref[...].astype(BF16) for ref in ckv_refs], axis=0)
    r_all = jnp.concatenate([ref[...].astype(BF16) for ref in kr_refs], axis=0)
    update(c_all, r_all, None)

    @pl.when(g == pl.num_programs(1) - 1)
    def _():
        shape = (rows, cnew_ref.shape[0])
        qstep = lax.broadcasted_iota(jnp.int32, shape, 0) // nh
        kidx = lax.broadcasted_iota(jnp.int32, shape, 1)
        update(cnew_ref[...].astype(BF16), rnew_ref[...].astype(BF16), kidx <= qstep)
        o_ref[...] = (acc_s[...] / l_s[...]).reshape(sq, nh, KV_LORA).astype(o_ref.dtype)


def sample_attention(qlat, qr_parts, ckv_new, kr_new, cache_ckv, cache_krope, page_table, pages_per_step=16):
    db, sq, nh, _ = qlat.shape
    n_pages = page_table.shape[1]
    n_steps = n_pages // pages_per_step
    rows = sq * nh

    def page_spec(width, i):
        return pl.BlockSpec((None, None, PAGE_SIZE, width),
                            lambda b, g, pt: (0, pt[b, g * pages_per_step + i], 0, 0))

    per_batch4 = lambda width: pl.BlockSpec((None, sq, nh, width), lambda b, g, pt: (b, 0, 0, 0))
    per_batch3 = lambda width: pl.BlockSpec((None, PAGE_SIZE, width), lambda b, g, pt: (b, 0, 0))
    grid_spec = pltpu.PrefetchScalarGridSpec(
        num_scalar_prefetch=1,
        grid=(db, n_steps),
        in_specs=([per_batch4(KV_LORA), per_batch4(2 * QK_ROPE), per_batch3(KV_LORA), per_batch3(QK_ROPE)]
                  + [page_spec(KV_LORA, i) for i in range(pages_per_step)]
                  + [page_spec(QK_ROPE, i) for i in range(pages_per_step)]),
        out_specs=per_batch4(KV_LORA),
        scratch_shapes=[pltpu.VMEM((rows, QK_ROPE), BF16), pltpu.VMEM((rows, 1), F32),
                        pltpu.VMEM((rows, 1), F32), pltpu.VMEM((rows, KV_LORA), F32)],
    )
    return pl.pallas_call(
        functools.partial(_sample_attn_kernel, pages_per_step=pages_per_step),
        grid_spec=grid_spec,
        out_shape=jax.ShapeDtypeStruct((db, sq, nh, KV_LORA), BF16),
        compiler_params=_params("parallel", "arbitrary"),
        name="sample_attention",
    )(page_table, qlat, qr_parts, ckv_new, kr_new,
      *([cache_ckv] * pages_per_step), *([cache_krope] * pages_per_step))


def _rotate_half_cols(w):
    half = QK_ROPE // 2
    return jnp.concatenate([-w[..., half:], w[..., :half]], axis=-1)


def _rope_tables(pos):
    inv = ROPE_THETA ** (-jnp.arange(0, QK_ROPE, 2, dtype=F32) / QK_ROPE)
    ang = pos.astype(F32)[:, None] * inv[None, :]
    cos, sin = jnp.cos(ang), jnp.sin(ang)
    tab_k = jnp.concatenate([cos, cos, sin, sin], axis=-1)
    tab_q = jnp.concatenate([jnp.ones((pos.shape[0], QK_NOPE), F32), tab_k], axis=-1)
    return tab_k, tab_q


def kernel(x_prompt, x_sample, cache_ckv, cache_krope, page_table, state_h, state_conv, norm_pre, norm_post, rec_w_in, rec_conv_w, rec_conv_b, rec_w_a, rec_b_a, rec_w_i, rec_b_i, rec_lambda, rec_w_out, mla_w_in, mla_q_norm, mla_kv_norm, mla_w_qb, mla_w_uk, mla_w_uv, mla_w_out):
    bsz, s, d = x_prompt.shape
    db, sq, _ = x_sample.shape
    tp, ts = bsz * s, db * sq

    w_in0 = rec_w_in[0].astype(BF16)
    w_out0 = rec_w_out[0].astype(BF16)
    cw, cb = rec_conv_w[0], rec_conv_b[0].reshape(1, LRU_WIDTH)
    wa, wi = rec_w_a[0].astype(BF16), rec_w_i[0].astype(BF16)
    ba = rec_b_a[0].reshape(LRU_BLOCKS, 1, LRU_BLOCK_W)
    bi = rec_b_i[0].reshape(LRU_BLOCKS, 1, LRU_BLOCK_W)
    lam = rec_lambda[0].reshape(1, LRU_WIDTH)

    split = Q_LORA + KV_LORA
    w_in1 = mla_w_in[0]
    w_kr = w_in1[:, split:split + QK_ROPE]
    w_a1 = jnp.concatenate([w_in1[:, :split + QK_ROPE], _rotate_half_cols(w_kr)], axis=1).astype(BF16)
    w_g1 = w_in1[:, split + QK_ROPE:].astype(BF16)
    w_qb = mla_w_qb[0]
    w_q = jnp.concatenate([w_qb, _rotate_half_cols(w_qb[..., QK_NOPE:])], axis=-1)
    w_q = w_q.reshape(Q_LORA, N_HEADS * HEAD_QW).astype(BF16)
    w_uk, w_uv = mla_w_uk[0], mla_w_uv[0]
    w_kv = jnp.concatenate([w_uk.reshape(KV_LORA, -1), w_uv.reshape(KV_LORA, -1)], axis=1).astype(BF16)
    w_ukt = w_uk.transpose(1, 2, 0).astype(BF16)
    w_uvh = w_uv.transpose(1, 0, 2).astype(BF16)
    w_out1 = mla_w_out[0].astype(BF16)

    tabk_p, tabq_p = _rope_tables(jnp.arange(s))
    tabk_s, tabq_s = _rope_tables(PAST_LEN + jnp.arange(sq))
    tabk_s, tabq_s = jnp.tile(tabk_s, (db, 1)), jnp.tile(tabq_s, (db, 1))

    xp = x_prompt.reshape(tp, d)
    xs = x_sample.transpose(1, 0, 2).reshape(ts, d)

    proj_p = matmul(rmsnorm(xp, norm_pre[0], BF16), w_in0, F32, 1024, 1024, name="rec_in_p")
    proj_s = matmul(rmsnorm(xs, norm_pre[0], BF16), w_in0, F32, ts, 1024, name="rec_in_s")
    proj_p3 = proj_p.reshape(bsz, s, 2 * LRU_WIDTH)
    proj_s3 = proj_s.reshape(sq, db, 2 * LRU_WIDTH)
    z_p, h_p = rglru_prompt(proj_p3, cw, cb, wa, ba, wi, bi, lam)
    z_s, h_s = rglru_sample(proj_s3, state_conv[0].transpose(1, 0, 2), state_h[0], cw, cb, wa, ba, wi, bi, lam)
    conv_p = proj_p3[:, s - (CONV_W - 1):, :LRU_WIDTH]
    conv_s = proj_s3[sq - (CONV_W - 1):, :, :LRU_WIDTH].transpose(1, 0, 2)
    y_p = matmul(z_p.reshape(tp, LRU_WIDTH), w_out0, F32, 1024, 1024, name="rec_out_p")
    y_s = matmul(z_s.reshape(ts, LRU_WIDTH), w_out0, F32, ts, 1024, name="rec_out_s")
    xp = norm_residual(y_p, norm_post[0], xp)
    xs = norm_residual(y_s, norm_post[0], xs)
    xs = xs.reshape(sq, db, d).transpose(1, 0, 2).reshape(ts, d)

    def project(x, tabk, tabq, tm):
        h = rmsnorm(x, norm_pre[1], BF16)
        pa = matmul(h, w_a1, F32, 512, w_a1.shape[1], name="mla_in_a")
        gate = matmul(h, w_g1, F32, tm, 1024, name="mla_in_gate")
        cqn, ckv, kr, kr2 = mla_mid(pa, mla_q_norm[0], mla_kv_norm[0], tabk)
        q = matmul(cqn, w_q, BF16, tm, 1024, table=tabq, name="mla_q")
        return q, ckv, kr, kr2, gate

    q_p, ckv_p, kr_p, kr2_p, gate_p = project(xp, tabk_p, tabq_p, 1024)
    q_s, ckv_s, kr_s, _, gate_s = project(xs, tabk_s, tabq_s, ts)

    kv_p = matmul(ckv_p, w_kv, BF16, 1024, 2048, name="mla_kv_up")
    z_p = prompt_attention(q_p.reshape(bsz, s, -1), kv_p.reshape(bsz, s, -1),
                           kr2_p.reshape(bsz, s, -1), gate_p.reshape(bsz, s, -1))

    qlat = absorb_queries(q_s, w_ukt).reshape(db, sq, N_HEADS, KV_LORA)
    qr_parts = q_s.reshape(db, sq, N_HEADS, HEAD_QW)[..., QK_NOPE:]
    pad = ((0, 0), (0, PAGE_SIZE - sq), (0, 0))
    out_lat = sample_attention(qlat, qr_parts,
                               jnp.pad(ckv_s.reshape(db, sq, KV_LORA), pad),
                               jnp.pad(kr_s.reshape(db, sq, QK_ROPE), pad),
                               cache_ckv, cache_krope, page_table)
    z_s = expand_values(out_lat.reshape(ts, N_HEADS * KV_LORA), w_uvh, gate_s)

    y_p = matmul(z_p.reshape(tp, GATE_W), w_out1, F32, 1024, 1024, name="mla_out_p")
    y_s = matmul(z_s, w_out1, F32, ts, 1024, name="mla_out_s")
    xp = norm_residual(y_p, norm_post[1], xp)
    xs = norm_residual(y_s, norm_post[1], xs)

    return (xp.reshape(bsz, s, d), xs.reshape(db, sq, d),
            ckv_p.reshape(1, bsz, s, KV_LORA), kr_p.reshape(1, bsz, s, QK_ROPE),
            ckv_s.reshape(1, db, sq, KV_LORA), kr_s.reshape(1, db, sq, QK_ROPE),
            h_p[None], conv_p[None], h_s[None], conv_s[None])
```

```python
import functools

import jax
import jax.numpy as jnp
from jax import lax
from jax.experimental import pallas as pl
from jax.experimental.pallas import tpu as pltpu

F32 = jnp.float32
BF16 = jnp.bfloat16

D_MODEL = 4096
LRU_WIDTH = 4096
LRU_BLOCKS = 16
LRU_BLOCK_W = LRU_WIDTH // LRU_BLOCKS
CONV_W = 4
LRU_C = 8.0
N_HEADS = 32
QK_NOPE = 128
QK_ROPE = 64
V_DIM = 128
Q_LORA = 1024
KV_LORA = 512
GATE_W = N_HEADS * V_DIM
PAST_LEN = 16384
PAGE_SIZE = 128
ROPE_THETA = 10000.0
ATTN_SCALE = (QK_NOPE + QK_ROPE) ** -0.5
NORM_EPS = 1e-6
HEAD_QW = QK_NOPE + 2 * QK_ROPE

VMEM_LIMIT_BYTES = 56 * 1024 * 1024
SUBLANES = 8
NEG_INF = float("-inf")


def _params(*sem):
    return pltpu.CompilerParams(dimension_semantics=sem, vmem_limit_bytes=VMEM_LIMIT_BYTES)


def _silu(x):
    return x * jax.nn.sigmoid(x)


def _rms(x, g):
    ms = jnp.mean(x * x, axis=-1, keepdims=True)
    return x * lax.rsqrt(ms + NORM_EPS) * g


def _rmsnorm_kernel(x_ref, g_ref, o_ref):
    o_ref[...] = _rms(x_ref[...], g_ref[...]).astype(o_ref.dtype)


def rmsnorm(x, g, out_dtype, tm=256):
    m, d = x.shape
    return pl.pallas_call(
        _rmsnorm_kernel,
        grid=(m // tm,),
        in_specs=[pl.BlockSpec((tm, d), lambda i: (i, 0)),
                  pl.BlockSpec((1, d), lambda i: (0, 0))],
        out_specs=pl.BlockSpec((tm, d), lambda i: (i, 0)),
        out_shape=jax.ShapeDtypeStruct((m, d), out_dtype),
        compiler_params=_params("parallel"),
        name="rmsnorm",
    )(x, g.reshape(1, d))


def _norm_residual_kernel(y_ref, g_ref, x_ref, o_ref):
    o_ref[...] = x_ref[...] + _rms(y_ref[...], g_ref[...])


def norm_residual(y, g, x, tm=256):
    m, d = x.shape
    row = pl.BlockSpec((tm, d), lambda i: (i, 0))
    return pl.pallas_call(
        _norm_residual_kernel,
        grid=(m // tm,),
        in_specs=[row, pl.BlockSpec((1, d), lambda i: (0, 0)), row],
        out_specs=row,
        out_shape=jax.ShapeDtypeStruct((m, d), F32),
        compiler_params=_params("parallel"),
        name="norm_residual",
    )(y, g.reshape(1, d), x)


def _mm_kernel(x_ref, w_ref, o_ref):
    x = x_ref[...].astype(BF16)
    o_ref[...] = jnp.dot(x, w_ref[...], preferred_element_type=F32).astype(o_ref.dtype)


def _mm_table_kernel(x_ref, w_ref, tab_ref, o_ref):
    x = x_ref[...].astype(BF16)
    acc = jnp.dot(x, w_ref[...], preferred_element_type=F32)
    tab = tab_ref[...]
    for j in range(acc.shape[1] // HEAD_QW):
        cols = slice(j * HEAD_QW, (j + 1) * HEAD_QW)
        o_ref[:, cols] = (acc[:, cols] * tab).astype(o_ref.dtype)


def matmul(x, w, out_dtype, tm, tn, table=None, name="matmul"):
    m, k = x.shape
    n = w.shape[1]
    in_specs = [pl.BlockSpec((tm, k), lambda i, j: (i, 0)),
                pl.BlockSpec((k, tn), lambda i, j: (0, j))]
    args = [x, w]
    kernel = _mm_kernel
    if table is not None:
        n_tab = table.shape[0] // tm
        in_specs.append(pl.BlockSpec((tm, HEAD_QW), lambda i, j: (i % n_tab, 0)))
        args.append(table)
        kernel = _mm_table_kernel
    return pl.pallas_call(
        kernel,
        grid=(m // tm, n // tn),
        in_specs=in_specs,
        out_specs=pl.BlockSpec((tm, tn), lambda i, j: (i, j)),
        out_shape=jax.ShapeDtypeStruct((m, n), out_dtype),
        compiler_params=_params("parallel", "parallel"),
        name=name,
    )(*args)


def _lru_coeffs(xc, wa_ref, ba_ref, wi_ref, bi_ref, lam_ref):
    xcb = xc.astype(BF16)
    r = jax.nn.sigmoid(jnp.dot(xcb, wa_ref[...], preferred_element_type=F32) + ba_ref[...])
    i = jax.nn.sigmoid(jnp.dot(xcb, wi_ref[...], preferred_element_type=F32) + bi_ref[...])
    log_a = -LRU_C * r * jax.nn.softplus(-lam_ref[...])
    a = jnp.exp(log_a)
    b = jnp.sqrt(-jnp.tanh(log_a) * (a * a + 1.0)) * i * xc
    return a, b


def _rglru_prompt_kernel(xb_ref, gate_ref, cw_ref, cb_ref, wa_ref, ba_ref, wi_ref, bi_ref, lam_ref,
                         z_ref, hlast_ref, xpad_s, hcar_s):
    t = pl.program_id(2)
    tc = xb_ref.shape[0]
    tail = CONV_W - 1

    @pl.when(t == 0)
    def _():
        xpad_s[0:SUBLANES, :] = jnp.zeros((SUBLANES, LRU_BLOCK_W), F32)
        hcar_s[...] = jnp.zeros_like(hcar_s)

    xb = xb_ref[...]
    xpad_s[SUBLANES:, :] = xb
    xc = cb_ref[...]
    for k in range(CONV_W):
        xc = xc + xpad_s[SUBLANES - tail + k:SUBLANES - tail + k + tc, :] * cw_ref[k:k + 1, :]
    xpad_s[0:SUBLANES, :] = xb[tc - SUBLANES:, :]

    a, b = _lru_coeffs(xc, wa_ref, ba_ref, wi_ref, bi_ref, lam_ref)

    n_groups = tc // SUBLANES
    a = a.reshape(n_groups, SUBLANES, LRU_BLOCK_W)
    b = b.reshape(n_groups, SUBLANES, LRU_BLOCK_W)
    row = lax.broadcasted_iota(jnp.int32, a.shape, 1)
    d = 1
    while d < SUBLANES:
        keep = row >= d
        a_prev = jnp.where(keep, pltpu.roll(a, d, 1), 1.0)
        b_prev = jnp.where(keep, pltpu.roll(b, d, 1), 0.0)
        b = a * b_prev + b
        a = a * a_prev
        d *= 2

    h = hcar_s[...]
    groups = []
    for g in range(n_groups):
        hg = a[g] * h + b[g]
        groups.append(hg)
        h = hg[SUBLANES - 1:SUBLANES]
    hs = jnp.concatenate(groups, axis=0)
    hcar_s[...] = h
    z_ref[...] = (hs * _silu(gate_ref[...])).astype(z_ref.dtype)

    @pl.when(t == pl.num_programs(2) - 1)
    def _():
        hlast_ref[...] = h


def rglru_prompt(proj, cw, cb, wa, ba, wi, bi, lam, tc=256):
    bsz, s, _ = proj.shape
    nb, bw = LRU_BLOCKS, LRU_BLOCK_W
    vec = lambda rows: pl.BlockSpec((rows, bw), lambda b, n, t: (0, n))
    blk = lambda rows: pl.BlockSpec((None, rows, bw), lambda b, n, t: (n, 0, 0))
    z, hlast = pl.pallas_call(
        _rglru_prompt_kernel,
        grid=(bsz, nb, s // tc),
        in_specs=[pl.BlockSpec((None, tc, bw), lambda b, n, t: (b, t, n)),
                  pl.BlockSpec((None, tc, bw), lambda b, n, t: (b, t, nb + n)),
                  vec(CONV_W), vec(1), blk(bw), blk(1), blk(bw), blk(1), vec(1)],
        out_specs=[pl.BlockSpec((None, tc, bw), lambda b, n, t: (b, t, n)),
                   pl.BlockSpec((None, 1, bw), lambda b, n, t: (b, 0, n))],
        out_shape=[jax.ShapeDtypeStruct((bsz, s, LRU_WIDTH), BF16),
                   jax.ShapeDtypeStruct((bsz, 1, LRU_WIDTH), F32)],
        scratch_shapes=[pltpu.VMEM((tc + SUBLANES, bw), F32), pltpu.VMEM((1, bw), F32)],
        compiler_params=_params("parallel", "parallel", "arbitrary"),
        name="rglru_prompt",
    )(proj, proj, cw, cb, wa, ba, wi, bi, lam)
    return z, hlast.reshape(bsz, LRU_WIDTH)


def _rglru_sample_kernel(xb_ref, gate_ref, cs_ref, h0_ref, cw_ref, cb_ref, wa_ref, ba_ref, wi_ref,
                         bi_ref, lam_ref, z_ref, hlast_ref):
    sq, db, bw = xb_ref.shape
    xpad = [cs_ref[k] for k in range(CONV_W - 1)] + [xb_ref[t] for t in range(sq)]
    xcs = []
    for t in range(sq):
        xc = cb_ref[...]
        for k in range(CONV_W):
            xc = xc + xpad[t + k] * cw_ref[k:k + 1, :]
        xcs.append(xc)
    a, b = _lru_coeffs(jnp.concatenate(xcs, axis=0), wa_ref, ba_ref, wi_ref, bi_ref, lam_ref)
    h = h0_ref[...]
    for t in range(sq):
        rows = slice(t * db, (t + 1) * db)
        h = a[rows] * h + b[rows]
        z_ref[t] = (h * _silu(gate_ref[t])).astype(z_ref.dtype)
    hlast_ref[...] = h


def rglru_sample(proj, conv_state, h0, cw, cb, wa, ba, wi, bi, lam):
    sq, db, _ = proj.shape
    nb, bw = LRU_BLOCKS, LRU_BLOCK_W
    vec = lambda rows: pl.BlockSpec((rows, bw), lambda n: (0, n))
    blk = lambda rows: pl.BlockSpec((None, rows, bw), lambda n: (n, 0, 0))
    return pl.pallas_call(
        _rglru_sample_kernel,
        grid=(nb,),
        in_specs=[pl.BlockSpec((sq, db, bw), lambda n: (0, 0, n)),
                  pl.BlockSpec((sq, db, bw), lambda n: (0, 0, nb + n)),
                  pl.BlockSpec((CONV_W - 1, db, bw), lambda n: (0, 0, n)),
                  vec(db), vec(CONV_W), vec(1), blk(bw), blk(1), blk(bw), blk(1), vec(1)],
        out_specs=[pl.BlockSpec((sq, db, bw), lambda n: (0, 0, n)), vec(db)],
        out_shape=[jax.ShapeDtypeStruct((sq, db, LRU_WIDTH), BF16),
                   jax.ShapeDtypeStruct((db, LRU_WIDTH), F32)],
        compiler_params=_params("parallel"),
        name="rglru_sample",
    )(proj, proj, conv_state, h0, cw, cb, wa, ba, wi, bi, lam)


def _mla_mid_kernel(p_ref, qn_ref, kvn_ref, tab_ref, cq_ref, ckv_ref, kr_ref, kr2_ref):
    cq_ref[...] = _rms(p_ref[:, :Q_LORA], qn_ref[...]).astype(cq_ref.dtype)
    ckv_ref[...] = _rms(p_ref[:, Q_LORA:Q_LORA + KV_LORA], kvn_ref[...])
    parts = p_ref[:, Q_LORA + KV_LORA:] * tab_ref[...]
    both = parts + pltpu.roll(parts, QK_ROPE, 1)
    kr_ref[...] = both[:, :QK_ROPE]
    kr2_ref[...] = both.astype(kr2_ref.dtype)


def mla_mid(p, q_norm, kv_norm, tab, tm=256):
    m, w = p.shape
    n_tab = tab.shape[0] // tm
    row = lambda width: pl.BlockSpec((tm, width), lambda i: (i, 0))
    return pl.pallas_call(
        _mla_mid_kernel,
        grid=(m // tm,),
        in_specs=[row(w),
                  pl.BlockSpec((1, Q_LORA), lambda i: (0, 0)),
                  pl.BlockSpec((1, KV_LORA), lambda i: (0, 0)),
                  pl.BlockSpec((tm, 2 * QK_ROPE), lambda i: (i % n_tab, 0))],
        out_specs=[row(Q_LORA), row(KV_LORA), row(QK_ROPE), row(2 * QK_ROPE)],
        out_shape=[jax.ShapeDtypeStruct((m, Q_LORA), BF16),
                   jax.ShapeDtypeStruct((m, KV_LORA), F32),
                   jax.ShapeDtypeStruct((m, QK_ROPE), F32),
                   jax.ShapeDtypeStruct((m, 2 * QK_ROPE), BF16)],
        compiler_params=_params("parallel"),
        name="mla_mid",
    )(p, q_norm.reshape(1, Q_LORA), kv_norm.reshape(1, KV_LORA), tab)


def _prompt_attn_kernel(q_ref, kn_ref, kr2_ref, v_ref, gate_ref, z_ref, k_s, *, tq):
    s = q_ref.shape[0]
    k_s[:, :QK_NOPE] = kn_ref[...]
    k_s[:, QK_NOPE:] = kr2_ref[...]
    contract_last = (((1,), (1,)), ((), ()))
    qpos = lax.broadcasted_iota(jnp.int32, (tq, tq), 0)
    kpos = lax.broadcasted_iota(jnp.int32, (tq, tq), 1)
    causal = kpos <= qpos

    for qi in range(s // tq):
        rows = slice(qi * tq, (qi + 1) * tq)
        past = slice(0, qi * tq)
        q = q_ref[rows, :]
        sc_d = lax.dot_general(q, k_s[rows, :], contract_last, preferred_element_type=F32)
        sc_d = jnp.where(causal, sc_d, NEG_INF)
        m = jnp.max(sc_d, axis=-1, keepdims=True)
        if qi:
            sc_p = lax.dot_general(q, k_s[past, :], contract_last, preferred_element_type=F32)
            m = jnp.maximum(m, jnp.max(sc_p, axis=-1, keepdims=True))
        p_d = jnp.exp((sc_d - m) * ATTN_SCALE)
        l = jnp.sum(p_d, axis=-1, keepdims=True)
        o = jnp.dot(p_d.astype(BF16), v_ref[rows, :], preferred_element_type=F32)
        if qi:
            p_p = jnp.exp((sc_p - m) * ATTN_SCALE)
            l = l + jnp.sum(p_p, axis=-1, keepdims=True)
            o = o + jnp.dot(p_p.astype(BF16), v_ref[past, :], preferred_element_type=F32)
        z_ref[rows, :] = (o / l * _silu(gate_ref[rows, :])).astype(z_ref.dtype)


def prompt_attention(q, kv, kr2, gate, tq=256):
    bsz, s, _ = q.shape
    head = lambda width, off: pl.BlockSpec((None, s, width), lambda b, h: (b, 0, off + h))
    return pl.pallas_call(
        functools.partial(_prompt_attn_kernel, tq=tq),
        grid=(bsz, N_HEADS),
        in_specs=[head(HEAD_QW, 0), head(QK_NOPE, 0),
                  pl.BlockSpec((None, s, 2 * QK_ROPE), lambda b, h: (b, 0, 0)),
                  head(V_DIM, N_HEADS), head(V_DIM, 0)],
        out_specs=head(V_DIM, 0),
        out_shape=jax.ShapeDtypeStruct((bsz, s, GATE_W), BF16),
        scratch_shapes=[pltpu.VMEM((s, HEAD_QW), BF16)],
        compiler_params=_params("parallel", "parallel"),
        name="prompt_attention",
    )(q, kv, kr2, kv, gate)


def _head_in_kernel(x_ref, w_ref, o_ref):
    o_ref[...] = jnp.dot(x_ref[...], w_ref[...], preferred_element_type=F32).astype(o_ref.dtype)


def absorb_queries(q, w_ukt):
    m = q.shape[0]
    return pl.pallas_call(
        _head_in_kernel,
        grid=(N_HEADS,),
        in_specs=[pl.BlockSpec((m, QK_NOPE), lambda h: (0, (HEAD_QW // QK_NOPE) * h)),
                  pl.BlockSpec((None, QK_NOPE, KV_LORA), lambda h: (h, 0, 0))],
        out_specs=pl.BlockSpec((m, KV_LORA), lambda h: (0, h)),
        out_shape=jax.ShapeDtypeStruct((m, N_HEADS * KV_LORA), BF16),
        compiler_params=_params("parallel"),
        name="absorb_queries",
    )(q, w_ukt)


def _head_out_kernel(x_ref, w_ref, gate_ref, z_ref):
    o = jnp.dot(x_ref[...], w_ref[...], preferred_element_type=F32)
    z_ref[...] = (o * _silu(gate_ref[...])).astype(z_ref.dtype)


def expand_values(out_lat, w_uvh, gate):
    m = out_lat.shape[0]
    return pl.pallas_call(
        _head_out_kernel,
        grid=(N_HEADS,),
        in_specs=[pl.BlockSpec((m, KV_LORA), lambda h: (0, h)),
                  pl.BlockSpec((None, KV_LORA, V_DIM), lambda h: (h, 0, 0)),
                  pl.BlockSpec((m, V_DIM), lambda h: (0, h))],
        out_specs=pl.BlockSpec((m, V_DIM), lambda h: (0, h)),
        out_shape=jax.ShapeDtypeStruct((m, GATE_W), BF16),
        compiler_params=_params("parallel"),
        name="expand_values",
    )(out_lat, w_uvh, gate)


def _sample_attn_kernel(pt_ref, qlat_ref, qr_ref, cnew_ref, rtnew_ref, *rest, pages_per_step, n_chains):
    del pt_ref
    ckv_refs = rest[:pages_per_step]
    krt_refs = rest[pages_per_step:2 * pages_per_step]
    o_ref, qr_s, m_s, l_s, acc_s = rest[2 * pages_per_step:]
    g = pl.program_id(1)
    sq, nh, _ = qlat_ref.shape
    rows = sq * nh

    @pl.when(g == 0)
    def _():
        parts = qr_ref[...].reshape(rows, 2 * QK_ROPE).astype(F32)
        qr_s[...] = (parts[:, :QK_ROPE] + parts[:, QK_ROPE:]).astype(qr_s.dtype)
        m_s[...] = jnp.full_like(m_s, NEG_INF)
        l_s[...] = jnp.zeros_like(l_s)
        acc_s[...] = jnp.zeros_like(acc_s)

    qlat = qlat_ref[...].reshape(rows, KV_LORA)
    qr = qr_s[...]

    def scores(c, rt):
        return (lax.dot_general(qlat, c, (((1,), (1,)), ((), ())), preferred_element_type=F32)
                + jnp.dot(qr, rt, preferred_element_type=F32))

    def accumulate(chain, sc, c):
        m_old = m_s[chain]
        m_new = jnp.maximum(m_old, jnp.max(sc, axis=-1, keepdims=True))
        alpha = jnp.exp((m_old - m_new) * ATTN_SCALE)
        p = jnp.exp((sc - m_new) * ATTN_SCALE)
        l_s[chain] = alpha * l_s[chain] + jnp.sum(p, axis=-1, keepdims=True)
        acc_s[chain] = alpha * acc_s[chain] + jnp.dot(p.astype(BF16), c, preferred_element_type=F32)
        m_s[chain] = m_new

    per_chain = pages_per_step // n_chains

    def load(chain):
        pages = range(chain * per_chain, (chain + 1) * per_chain)
        c = jnp.concatenate([ckv_refs[i][...].astype(BF16) for i in pages], axis=0)
        rt = jnp.concatenate([krt_refs[i][...].astype(BF16) for i in pages], axis=1)
        return c, rt

    c_cur, rt_cur = load(0)
    sc_cur = scores(c_cur, rt_cur)
    for chain in range(1, n_chains):
        c_nxt, rt_nxt = load(chain)
        sc_nxt = scores(c_nxt, rt_nxt)
        accumulate(chain - 1, sc_cur, c_cur)
        c_cur, sc_cur = c_nxt, sc_nxt
    accumulate(n_chains - 1, sc_cur, c_cur)

    @pl.when(g == pl.num_programs(1) - 1)
    def _():
        shape = (rows, cnew_ref.shape[0])
        qstep = lax.broadcasted_iota(jnp.int32, shape, 0) // nh
        kidx = lax.broadcasted_iota(jnp.int32, shape, 1)
        c_new = cnew_ref[...].astype(BF16)
        sc_new = scores(c_new, rtnew_ref[...].astype(BF16))
        accumulate(0, jnp.where(kidx <= qstep, sc_new, NEG_INF), c_new)
        m = m_s[0]
        for chain in range(1, n_chains):
            m = jnp.maximum(m, m_s[chain])
        l = jnp.zeros_like(m)
        acc = jnp.zeros((rows, KV_LORA), F32)
        for chain in range(n_chains):
            w = jnp.exp((m_s[chain] - m) * ATTN_SCALE)
            l = l + w * l_s[chain]
            acc = acc + w * acc_s[chain]
        o_ref[...] = (acc / l).reshape(sq, nh, KV_LORA).astype(o_ref.dtype)


def sample_attention(qlat, qr_parts, ckv_new, krt_new, cache_ckv, cache_krope_t, page_table,
                     pages_per_step=16, n_chains=4):
    db, sq, nh, _ = qlat.shape
    n_pages = page_table.shape[1]
    n_steps = n_pages // pages_per_step
    rows = sq * nh

    def page_spec(shape, i):
        return pl.BlockSpec((None, None) + shape, lambda b, g, pt: (0, pt[b, g * pages_per_step + i], 0, 0))

    per_batch4 = lambda width: pl.BlockSpec((None, sq, nh, width), lambda b, g, pt: (b, 0, 0, 0))
    per_batch3 = lambda shape: pl.BlockSpec((None,) + shape, lambda b, g, pt: (b, 0, 0))
    grid_spec = pltpu.PrefetchScalarGridSpec(
        num_scalar_prefetch=1,
        grid=(db, n_steps),
        in_specs=([per_batch4(KV_LORA), per_batch4(2 * QK_ROPE),
                   per_batch3((PAGE_SIZE, KV_LORA)), per_batch3((QK_ROPE, PAGE_SIZE))]
                  + [page_spec((PAGE_SIZE, KV_LORA), i) for i in range(pages_per_step)]
                  + [page_spec((QK_ROPE, PAGE_SIZE), i) for i in range(pages_per_step)]),
        out_specs=per_batch4(KV_LORA),
        scratch_shapes=[pltpu.VMEM((rows, QK_ROPE), BF16), pltpu.VMEM((n_chains, rows, 1), F32),
                        pltpu.VMEM((n_chains, rows, 1), F32), pltpu.VMEM((n_chains, rows, KV_LORA), F32)],
    )
    return pl.pallas_call(
        functools.partial(_sample_attn_kernel, pages_per_step=pages_per_step, n_chains=n_chains),
        grid_spec=grid_spec,
        out_shape=jax.ShapeDtypeStruct((db, sq, nh, KV_LORA), BF16),
        compiler_params=_params("parallel", "arbitrary"),
        name="sample_attention",
    )(page_table, qlat, qr_parts, ckv_new, krt_new,
      *([cache_ckv] * pages_per_step), *([cache_krope_t] * pages_per_step))


def _rotate_half_cols(w):
    half = QK_ROPE // 2
    return jnp.concatenate([-w[..., half:], w[..., :half]], axis=-1)


def _rope_tables(pos):
    inv = ROPE_THETA ** (-jnp.arange(0, QK_ROPE, 2, dtype=F32) / QK_ROPE)
    ang = pos.astype(F32)[:, None] * inv[None, :]
    cos, sin = jnp.cos(ang), jnp.sin(ang)
    tab_k = jnp.concatenate([cos, cos, sin, sin], axis=-1)
    tab_q = jnp.concatenate([jnp.ones((pos.shape[0], QK_NOPE), F32), tab_k], axis=-1)
    return tab_k, tab_q


def kernel(x_prompt, x_sample, cache_ckv, cache_krope, page_table, state_h, state_conv, norm_pre, norm_post, rec_w_in, rec_conv_w, rec_conv_b, rec_w_a, rec_b_a, rec_w_i, rec_b_i, rec_lambda, rec_w_out, mla_w_in, mla_q_norm, mla_kv_norm, mla_w_qb, mla_w_uk, mla_w_uv, mla_w_out):
    bsz, s, d = x_prompt.shape
    db, sq, _ = x_sample.shape
    tp, ts = bsz * s, db * sq

    w_in0 = rec_w_in[0].astype(BF16)
    w_out0 = rec_w_out[0].astype(BF16)
    cw, cb = rec_conv_w[0], rec_conv_b[0].reshape(1, LRU_WIDTH)
    wa, wi = rec_w_a[0].astype(BF16), rec_w_i[0].astype(BF16)
    ba = rec_b_a[0].reshape(LRU_BLOCKS, 1, LRU_BLOCK_W)
    bi = rec_b_i[0].reshape(LRU_BLOCKS, 1, LRU_BLOCK_W)
    lam = rec_lambda[0].reshape(1, LRU_WIDTH)

    split = Q_LORA + KV_LORA
    w_in1 = mla_w_in[0]
    w_kr = w_in1[:, split:split + QK_ROPE]
    w_a1 = jnp.concatenate([w_in1[:, :split + QK_ROPE], _rotate_half_cols(w_kr)], axis=1).astype(BF16)
    w_g1 = w_in1[:, split + QK_ROPE:].astype(BF16)
    w_qb = mla_w_qb[0]
    w_q = jnp.concatenate([w_qb, _rotate_half_cols(w_qb[..., QK_NOPE:])], axis=-1)
    w_q = w_q.reshape(Q_LORA, N_HEADS * HEAD_QW).astype(BF16)
    w_uk, w_uv = mla_w_uk[0], mla_w_uv[0]
    w_kv = jnp.concatenate([w_uk.reshape(KV_LORA, -1), w_uv.reshape(KV_LORA, -1)], axis=1).astype(BF16)
    w_ukt = w_uk.transpose(1, 2, 0).astype(BF16)
    w_uvh = w_uv.transpose(1, 0, 2).astype(BF16)
    w_out1 = mla_w_out[0].astype(BF16)

    tabk_p, tabq_p = _rope_tables(jnp.arange(s))
    tabk_s, tabq_s = _rope_tables(PAST_LEN + jnp.arange(sq))
    tabk_s, tabq_s = jnp.tile(tabk_s, (db, 1)), jnp.tile(tabq_s, (db, 1))

    xp = x_prompt.reshape(tp, d)
    xs = x_sample.transpose(1, 0, 2).reshape(ts, d)

    proj_p = matmul(rmsnorm(xp, norm_pre[0], BF16), w_in0, F32, 1024, 1024, name="rec_in_p")
    proj_s = matmul(rmsnorm(xs, norm_pre[0], BF16), w_in0, F32, ts, 1024, name="rec_in_s")
    proj_p3 = proj_p.reshape(bsz, s, 2 * LRU_WIDTH)
    proj_s3 = proj_s.reshape(sq, db, 2 * LRU_WIDTH)
    z_p, h_p = rglru_prompt(proj_p3, cw, cb, wa, ba, wi, bi, lam)
    z_s, h_s = rglru_sample(proj_s3, state_conv[0].transpose(1, 0, 2), state_h[0], cw, cb, wa, ba, wi, bi, lam)
    conv_p = proj_p3[:, s - (CONV_W - 1):, :LRU_WIDTH]
    conv_s = proj_s3[sq - (CONV_W - 1):, :, :LRU_WIDTH].transpose(1, 0, 2)
    y_p = matmul(z_p.reshape(tp, LRU_WIDTH), w_out0, F32, 1024, 1024, name="rec_out_p")
    y_s = matmul(z_s.reshape(ts, LRU_WIDTH), w_out0, F32, ts, 1024, name="rec_out_s")
    xp = norm_residual(y_p, norm_post[0], xp)
    xs = norm_residual(y_s, norm_post[0], xs)
    xs = xs.reshape(sq, db, d).transpose(1, 0, 2).reshape(ts, d)

    def project(x, tabk, tabq, tm):
        h = rmsnorm(x, norm_pre[1], BF16)
        pa = matmul(h, w_a1, F32, 512, w_a1.shape[1], name="mla_in_a")
        gate = matmul(h, w_g1, F32, tm, 1024, name="mla_in_gate")
        cqn, ckv, kr, kr2 = mla_mid(pa, mla_q_norm[0], mla_kv_norm[0], tabk)
        q = matmul(cqn, w_q, BF16, tm, 1024, table=tabq, name="mla_q")
        return q, ckv, kr, kr2, gate

    q_p, ckv_p, kr_p, kr2_p, gate_p = project(xp, tabk_p, tabq_p, 1024)
    q_s, ckv_s, kr_s, _, gate_s = project(xs, tabk_s, tabq_s, ts)

    kv_p = matmul(ckv_p, w_kv, BF16, 1024, 2048, name="mla_kv_up")
    z_p = prompt_attention(q_p.reshape(bsz, s, -1), kv_p.reshape(bsz, s, -1),
                           kr2_p.reshape(bsz, s, -1), gate_p.reshape(bsz, s, -1))

    qlat = absorb_queries(q_s, w_ukt).reshape(db, sq, N_HEADS, KV_LORA)
    qr_parts = q_s.reshape(db, sq, N_HEADS, HEAD_QW)[..., QK_NOPE:]
    pad = ((0, 0), (0, PAGE_SIZE - sq), (0, 0))
    out_lat = sample_attention(qlat, qr_parts,
                               jnp.pad(ckv_s.reshape(db, sq, KV_LORA), pad),
                               jnp.pad(kr_s.reshape(db, sq, QK_ROPE), pad).transpose(0, 2, 1),
                               cache_ckv, cache_krope.transpose(0, 1, 3, 2), page_table)
    z_s = expand_values(out_lat.reshape(ts, N_HEADS * KV_LORA), w_uvh, gate_s)

    y_p = matmul(z_p.reshape(tp, GATE_W), w_out1, F32, 1024, 1024, name="mla_out_p")
    y_s = matmul(z_s, w_out1, F32, ts, 1024, name="mla_out_s")
    xp = norm_residual(y_p, norm_post[1], xp)
    xs = norm_residual(y_s, norm_post[1], xs)

    return (xp.reshape(bsz, s, d), xs.reshape(db, sq, d),
            ckv_p.reshape(1, bsz, s, KV_LORA), kr_p.reshape(1, bsz, s, QK_ROPE),
            ckv_s.reshape(1, db, sq, KV_LORA), kr_s.reshape(1, db, sq, QK_ROPE),
            h_p[None], conv_p[None], h_s[None], conv_s[None])
```
